```python
import jax
import jax.numpy as jnp
from jax import lax
import numpy as np

D_MODEL = 1024
BATCH = 2
SEQ = 8192
DEPTH = 4
DEC_BATCH = 128
DEC_SEQ = 1
PAST_LEN = 8192
PAGE_SIZE = 128

C_CONV = 512
CONV_W = 31
MLA_HEADS = 8
Q_LORA = 384
KV_LORA = 256
QK_NOPE = 64
QK_ROPE = 32
V_HEAD = 64
ROPE_BASE = 10000.0
SB_HEADS = 8
SB_KV_HEADS = 2
SB_GROUP = SB_HEADS // SB_KV_HEADS
SB_HEAD_DIM = 64
N_BRANCH = 3
D_FF = 2816
N_EXPERTS = 8
TOP_K = 2
D_EXPERT = 1792
N_DENSE = (DEPTH + 1) // 2
N_MOE = DEPTH // 2
Q_BLOCK = 128
EPS = 1e-6
IN_SIZES = (C_CONV, C_CONV, Q_LORA, KV_LORA, QK_ROPE, SB_HEADS * SB_HEAD_DIM, SB_KV_HEADS * SB_HEAD_DIM, SB_KV_HEADS * SB_HEAD_DIM, N_BRANCH * D_MODEL)
D_IN = sum(IN_SIZES)
MLA_SCALE = (QK_NOPE + QK_ROPE) ** -0.5
SB_SCALE = SB_HEAD_DIM ** -0.5

kernel_name = 'hybrid_conv_mla_stickbreak_decode_step'


def rms_norm(x, g):
    xf = x.astype(jnp.float32)
    y = xf * lax.rsqrt(jnp.mean(xf * xf, axis=-1, keepdims=True) + EPS)
    return (y * g.astype(jnp.float32)).astype(x.dtype)


def layer_norm(x, g, b):
    xf = x.astype(jnp.float32)
    xc = xf - jnp.mean(xf, axis=-1, keepdims=True)
    y = xc * lax.rsqrt(jnp.mean(xc * xc, axis=-1, keepdims=True) + EPS)
    return (y * g.astype(jnp.float32) + b.astype(jnp.float32)).astype(x.dtype)


def rope(x, pos):
    half = QK_ROPE // 2
    inv = ROPE_BASE ** (-jnp.arange(half, dtype=jnp.float32) / half)
    ang = pos.astype(jnp.float32)[:, None] * inv[None, :]
    ang = ang.reshape(ang.shape[0], *([1] * (x.ndim - 3)), half)
    cos = jnp.cos(ang).astype(x.dtype)
    sin = jnp.sin(ang).astype(x.dtype)
    x1, x2 = x[..., :half], x[..., half:]
    return jnp.concatenate([x1 * cos - x2 * sin, x1 * sin + x2 * cos], axis=-1)


def project_inputs(h, pos, w_in_l, q_norm_g_l, w_uq_l, kv_norm_g_l):
    B, T, _ = h.shape
    z = h @ w_in_l
    parts, off = [], 0
    for n in IN_SIZES:
        parts.append(z[..., off:off + n])
        off += n
    glu_a, glu_b, c_q, c_kv, k_rope, sb_q, sb_k, sb_v, g_raw = parts
    glu = glu_a * jax.nn.sigmoid(glu_b)
    q = (rms_norm(c_q, q_norm_g_l) @ w_uq_l).reshape(B, T, MLA_HEADS, QK_NOPE + QK_ROPE)
    q_nope = q[..., :QK_NOPE]
    q_rope = rope(q[..., QK_NOPE:], pos)
    c_kv = rms_norm(c_kv, kv_norm_g_l)
    k_rope = rope(k_rope, pos)
    sb_q = sb_q.reshape(B, T, SB_HEADS, SB_HEAD_DIM)
    sb_k = sb_k.reshape(B, T, SB_KV_HEADS, SB_HEAD_DIM)
    sb_v = sb_v.reshape(B, T, SB_KV_HEADS, SB_HEAD_DIM)
    gates = jax.nn.sigmoid(g_raw).reshape(B, T, N_BRANCH, D_MODEL)
    return glu, q_nope, q_rope, c_kv, k_rope, sb_q, sb_k, sb_v, gates


def conv_branch(buf, w, b, ln_g, ln_b):
    y = lax.conv_general_dilated(buf, w[:, None, :], window_strides=(1,), padding='VALID',
                                 dimension_numbers=('NWC', 'WIO', 'NWC'), feature_group_count=C_CONV)
    return jax.nn.silu(layer_norm(y + b, ln_g, ln_b))


def mla_prompt(q_nope, q_rope, c_kv, k_rope, w_uk, w_uv):
    B, S = c_kv.shape[:2]
    k_nope = jnp.einsum('bsc,chn->bshn', c_kv, w_uk)
    v = jnp.einsum('bsc,chv->bshv', c_kv, w_uv)
    k = jnp.concatenate([k_nope, jnp.broadcast_to(k_rope[:, :, None, :], (B, S, MLA_HEADS, QK_ROPE))], axis=-1)
    q = jnp.concatenate([q_nope, q_rope], axis=-1)
    nb = S // Q_BLOCK
    qb = q.reshape(B, nb, Q_BLOCK, MLA_HEADS, QK_NOPE + QK_ROPE).transpose(1, 0, 2, 3, 4)
    k_pos = jnp.arange(S)

    def block(args):
        qi, i = args
        q_pos = i * Q_BLOCK + jnp.arange(Q_BLOCK)
        s = jnp.einsum('bqhd,bshd->bhqs', qi, k).astype(jnp.float32) * MLA_SCALE
        s = jnp.where(k_pos[None, :] <= q_pos[:, None], s, -jnp.inf)
        p = jax.nn.softmax(s, axis=-1).astype(v.dtype)
        return jnp.einsum('bhqs,bshv->bqhv', p, v).reshape(B, Q_BLOCK, MLA_HEADS * V_HEAD)

    o = lax.map(block, (qb, jnp.arange(nb)))
    return o.transpose(1, 0, 2, 3).reshape(B, S, MLA_HEADS * V_HEAD)


def mla_sample(q_nope, q_rope, c_kv, k_rope, lat_past, kr_past, w_uk, w_uv):
    B, T = c_kv.shape[:2]
    P = lat_past.shape[1]
    lat = jnp.concatenate([lat_past, c_kv], axis=1)
    kr = jnp.concatenate([kr_past, k_rope], axis=1)
    q_lat = jnp.einsum('bthn,chn->bthc', q_nope, w_uk)
    s = (jnp.einsum('bthc,bsc->bhts', q_lat, lat) + jnp.einsum('bthr,bsr->bhts', q_rope, kr)).astype(jnp.float32) * MLA_SCALE
    q_pos = P + jnp.arange(T)
    k_pos = jnp.arange(P + T)
    s = jnp.where(k_pos[None, :] <= q_pos[:, None], s, -jnp.inf)
    p = jax.nn.softmax(s, axis=-1).astype(lat.dtype)
    o_lat = jnp.einsum('bhts,bsc->bthc', p, lat)
    return jnp.einsum('bthc,chv->bthv', o_lat, w_uv).reshape(B, T, MLA_HEADS * V_HEAD)


def stick_breaking(q, k, v, q_pos, k_pos):
    B, Tq = q.shape[:2]
    qg = q.reshape(B, Tq, SB_KV_HEADS, SB_GROUP, SB_HEAD_DIM)
    z = jnp.einsum('bqkgd,bskd->bkgqs', qg, k).astype(jnp.float32) * SB_SCALE
    mask = k_pos[None, :] < q_pos[:, None]
    log_rest = jnp.where(mask, jax.nn.log_sigmoid(-z), 0.0)
    rev = lax.cumsum(log_rest, axis=4, reverse=True)
    after = jnp.concatenate([rev[..., 1:], jnp.zeros_like(rev[..., :1])], axis=-1)
    a = jnp.where(mask, jnp.exp(jax.nn.log_sigmoid(z) + after), 0.0).astype(v.dtype)
    o = jnp.einsum('bkgqs,bskd->bqkgd', a, v)
    return o.reshape(B, Tq, SB_HEADS * SB_HEAD_DIM)


def sb_prompt(q, k, v):
    B, S = q.shape[:2]
    nb = S // Q_BLOCK
    qb = q.reshape(B, nb, Q_BLOCK, SB_HEADS, SB_HEAD_DIM).transpose(1, 0, 2, 3, 4)
    k_pos = jnp.arange(S)

    def block(args):
        qi, i = args
        return stick_breaking(qi, k, v, i * Q_BLOCK + jnp.arange(Q_BLOCK), k_pos)

    o = lax.map(block, (qb, jnp.arange(nb)))
    return o.transpose(1, 0, 2, 3).reshape(B, S, SB_HEADS * SB_HEAD_DIM)


def merge_branches(gates, conv_o, mla_o, sb_o, w_br_conv_l, w_br_mla_l, w_br_sb_l, w_out_l):
    m = (gates[:, :, 0] * (conv_o @ w_br_conv_l)
         + gates[:, :, 1] * (mla_o @ w_br_mla_l)
         + gates[:, :, 2] * (sb_o @ w_br_sb_l))
    return m @ w_out_l


def swiglu(h, wg, wu, wd):
    return (jax.nn.silu(h @ wg) * (h @ wu)) @ wd


def channel_mixer(h, l, ffn_w_gate, ffn_w_up, ffn_w_down, moe_router, moe_w_gate, moe_w_up, moe_w_down):
    i = l // 2
    if l % 2 == 0:
        return swiglu(h, ffn_w_gate[i], ffn_w_up[i], ffn_w_down[i])
    logits = (h @ moe_router[i]).astype(jnp.float32)
    top_v, top_i = lax.top_k(logits, TOP_K)
    w = jax.nn.softmax(top_v, axis=-1)
    gate = jnp.sum(jax.nn.one_hot(top_i, N_EXPERTS, dtype=jnp.float32) * w[..., None], axis=-2).astype(h.dtype)
    out = jnp.zeros_like(h)
    for e in range(N_EXPERTS):
        out = out + gate[..., e:e + 1] * swiglu(h, moe_w_gate[i, e], moe_w_up[i, e], moe_w_down[i, e])
    return out


def setup_inputs(seed: int = 0) -> dict:
    key = jax.random.key(seed)
    keys = list(jax.random.split(key, 40))

    def nrm(shape, scale):
        return jax.random.normal(keys.pop(), shape, jnp.float32) * scale

    n_pages = PAST_LEN // PAGE_SIZE
    n_used = DEC_BATCH * n_pages
    n_pool = n_used + n_used // 4
    page_table = jax.random.permutation(keys.pop(), n_pool)[:n_used].reshape(DEC_BATCH, n_pages).astype(jnp.int32)
    return {
        'x_prompt': nrm((BATCH, SEQ, D_MODEL), 1.0),
        'x_sample': nrm((DEC_BATCH, DEC_SEQ, D_MODEL), 1.0),
        'cache_mla_latent': nrm((DEPTH, n_pool, PAGE_SIZE, KV_LORA), 1.0),
        'cache_mla_krope': nrm((DEPTH, n_pool, PAGE_SIZE, QK_ROPE), 1.0),
        'cache_sb_k': nrm((DEPTH, n_pool, PAGE_SIZE, SB_KV_HEADS, SB_HEAD_DIM), 1.0),
        'cache_sb_v': nrm((DEPTH, n_pool, PAGE_SIZE, SB_KV_HEADS, SB_HEAD_DIM), 1.0),
        'state_conv': nrm((DEPTH, DEC_BATCH, CONV_W - 1, C_CONV), 0.5),
        'page_table': page_table,
        'norm_mix_g': 1.0 + nrm((DEPTH, D_MODEL), 0.02),
        'norm_ffn_g': 1.0 + nrm((DEPTH, D_MODEL), 0.02),
        'norm_final_g': 1.0 + nrm((D_MODEL,), 0.02),
        'w_in': nrm((DEPTH, D_MODEL, D_IN), D_MODEL ** -0.5),
        'conv_w': nrm((DEPTH, CONV_W, C_CONV), CONV_W ** -0.5),
        'conv_b': nrm((DEPTH, C_CONV), 0.02),
        'conv_ln_g': 1.0 + nrm((DEPTH, C_CONV), 0.02),
        'conv_ln_b': nrm((DEPTH, C_CONV), 0.02),
        'mla_q_norm_g': 1.0 + nrm((DEPTH, Q_LORA), 0.02),
        'mla_w_uq': nrm((DEPTH, Q_LORA, MLA_HEADS * (QK_NOPE + QK_ROPE)), Q_LORA ** -0.5),
        'mla_kv_norm_g': 1.0 + nrm((DEPTH, KV_LORA), 0.02),
        'mla_w_uk': nrm((DEPTH, KV_LORA, MLA_HEADS, QK_NOPE), KV_LORA ** -0.5),
        'mla_w_uv': nrm((DEPTH, KV_LORA, MLA_HEADS, V_HEAD), KV_LORA ** -0.5),
        'w_br_conv': nrm((DEPTH, C_CONV, D_MODEL), C_CONV ** -0.5),
        'w_br_mla': nrm((DEPTH, MLA_HEADS * V_HEAD, D_MODEL), (MLA_HEADS * V_HEAD) ** -0.5),
        'w_br_sb': nrm((DEPTH, SB_HEADS * SB_HEAD_DIM, D_MODEL), (SB_HEADS * SB_HEAD_DIM) ** -0.5),
        'w_out': nrm((DEPTH, D_MODEL, D_MODEL), D_MODEL ** -0.5),
        'ffn_w_gate': nrm((N_DENSE, D_MODEL, D_FF), D_MODEL ** -0.5),
        'ffn_w_up': nrm((N_DENSE, D_MODEL, D_FF), D_MODEL ** -0.5),
        'ffn_w_down': nrm((N_DENSE, D_FF, D_MODEL), D_FF ** -0.5),
        'moe_router': nrm((N_MOE, D_MODEL, N_EXPERTS), D_MODEL ** -0.5),
        'moe_w_gate': nrm((N_MOE, N_EXPERTS, D_MODEL, D_EXPERT), D_MODEL ** -0.5),
        'moe_w_up': nrm((N_MOE, N_EXPERTS, D_MODEL, D_EXPERT), D_MODEL ** -0.5),
        'moe_w_down': nrm((N_MOE, N_EXPERTS, D_EXPERT, D_MODEL), D_EXPERT ** -0.5),
    }


def reference(x_prompt, x_sample, cache_mla_latent, cache_mla_krope, cache_sb_k, cache_sb_v, state_conv, page_table,
              norm_mix_g, norm_ffn_g, norm_final_g, w_in, conv_w, conv_b, conv_ln_g, conv_ln_b,
              mla_q_norm_g, mla_w_uq, mla_kv_norm_g, mla_w_uk, mla_w_uv,
              w_br_conv, w_br_mla, w_br_sb, w_out,
              ffn_w_gate, ffn_w_up, ffn_w_down, moe_router, moe_w_gate, moe_w_up, moe_w_down):
    n_dec, t_dec = x_sample.shape[:2]
    pos_p = jnp.arange(x_prompt.shape[1])
    pos_s = PAST_LEN + jnp.arange(t_dec)
    xp, xs = x_prompt, x_sample
    lat_p, kr_p, sbk_p, sbv_p, cst_p = [], [], [], [], []
    lat_s, kr_s, sbk_s, sbv_s, cst_s = [], [], [], [], []
    for l in range(DEPTH):
        hp = rms_norm(xp, norm_mix_g[l])
        hs = rms_norm(xs, norm_mix_g[l])
        glu_p, qn_p, qr_p, ckv_p, krp_p, sq_p, sk_p, sv_p, g_p = project_inputs(hp, pos_p, w_in[l], mla_q_norm_g[l], mla_w_uq[l], mla_kv_norm_g[l])
        glu_s, qn_s, qr_s, ckv_s, krp_s, sq_s, sk_s, sv_s, g_s = project_inputs(hs, pos_s, w_in[l], mla_q_norm_g[l], mla_w_uq[l], mla_kv_norm_g[l])

        buf_p = jnp.pad(glu_p, ((0, 0), (CONV_W - 1, 0), (0, 0)))
        buf_s = jnp.concatenate([state_conv[l], glu_s], axis=1)
        conv_p = conv_branch(buf_p, conv_w[l], conv_b[l], conv_ln_g[l], conv_ln_b[l])
        conv_s = conv_branch(buf_s, conv_w[l], conv_b[l], conv_ln_g[l], conv_ln_b[l])

        mla_p = mla_prompt(qn_p, qr_p, ckv_p, krp_p, mla_w_uk[l], mla_w_uv[l])
        lat_past = cache_mla_latent[l, page_table].reshape(n_dec, -1, KV_LORA)
        kr_past = cache_mla_krope[l, page_table].reshape(n_dec, -1, QK_ROPE)
        mla_s = mla_sample(qn_s, qr_s, ckv_s, krp_s, lat_past, kr_past, mla_w_uk[l], mla_w_uv[l])

        sb_p = sb_prompt(sq_p, sk_p, sv_p)
        k_all = jnp.concatenate([cache_sb_k[l, page_table].reshape(n_dec, -1, SB_KV_HEADS, SB_HEAD_DIM), sk_s], axis=1)
        v_all = jnp.concatenate([cache_sb_v[l, page_table].reshape(n_dec, -1, SB_KV_HEADS, SB_HEAD_DIM), sv_s], axis=1)
        sb_s = stick_breaking(sq_s, k_all, v_all, pos_s, jnp.arange(k_all.shape[1]))

        xp = xp + merge_branches(g_p, conv_p, mla_p, sb_p, w_br_conv[l], w_br_mla[l], w_br_sb[l], w_out[l])
        xs = xs + merge_branches(g_s, conv_s, mla_s, sb_s, w_br_conv[l], w_br_mla[l], w_br_sb[l], w_out[l])
        xp = xp + channel_mixer(rms_norm(xp, norm_ffn_g[l]), l, ffn_w_gate, ffn_w_up, ffn_w_down, moe_router, moe_w_gate, moe_w_up, moe_w_down)
        xs = xs + channel_mixer(rms_norm(xs, norm_ffn_g[l]), l, ffn_w_gate, ffn_w_up, ffn_w_down, moe_router, moe_w_gate, moe_w_up, moe_w_down)

        lat_p.append(ckv_p)
        kr_p.append(krp_p)
        sbk_p.append(sk_p)
        sbv_p.append(sv_p)
        cst_p.append(buf_p[:, -(CONV_W - 1):])
        lat_s.append(ckv_s)
        kr_s.append(krp_s)
        sbk_s.append(sk_s)
        sbv_s.append(sv_s)
        cst_s.append(buf_s[:, -(CONV_W - 1):])

    y_prompt = rms_norm(xp, norm_final_g)
    y_sample = rms_norm(xs, norm_final_g)
    return (y_prompt, y_sample,
            jnp.stack(lat_p), jnp.stack(kr_p), jnp.stack(sbk_p), jnp.stack(sbv_p), jnp.stack(cst_p),
            jnp.stack(lat_s), jnp.stack(kr_s), jnp.stack(sbk_s), jnp.stack(sbv_s), jnp.stack(cst_s))
```

```python
import functools

import numpy as np
import jax
import jax.numpy as jnp
from jax import lax
from jax.experimental import pallas as pl
from jax.experimental.pallas import tpu as pltpu

D_MODEL = 1024
DEPTH = 4
PAST_LEN = 8192
PAGE_SIZE = 128
C_CONV = 512
CONV_W = 31
MLA_HEADS = 8
Q_LORA = 384
KV_LORA = 256
QK_NOPE = 64
QK_ROPE = 32
V_HEAD = 64
ROPE_BASE = 10000.0
SB_HEADS = 8
SB_KV_HEADS = 2
SB_GROUP = SB_HEADS // SB_KV_HEADS
SB_HEAD_DIM = 64
N_BRANCH = 3
D_FF = 2816
N_EXPERTS = 8
D_EXPERT = 1792
EPS = 1e-6
IN_SIZES = (C_CONV, C_CONV, Q_LORA, KV_LORA, QK_ROPE, SB_HEADS * SB_HEAD_DIM, SB_KV_HEADS * SB_HEAD_DIM,
            SB_KV_HEADS * SB_HEAD_DIM, N_BRANCH * D_MODEL)
MLA_SCALE = (QK_NOPE + QK_ROPE) ** -0.5
SB_SCALE = SB_HEAD_DIM ** -0.5

F32 = jnp.float32
BF16 = jnp.bfloat16
LANES = 128
SUBLANES = 8
HALF = QK_ROPE // 2
NT_DIMS = (((1,), (1,)), ((), ()))
NN_DIMS = (((1,), (0,)), ((), ()))
V7X_VMEM_BYTES = 64 * 2**20
VMEM_LIMIT = V7X_VMEM_BYTES - 8 * 2**20
SB_DEAD_LOG = -104.0

P_GLU = 0
P_CQ = P_GLU + 2 * C_CONV
P_CKV = P_CQ + Q_LORA
P_SBQ = P_CKV + KV_LORA
P_SBK = P_SBQ + SB_HEADS * SB_HEAD_DIM
P_SBV = P_SBK + SB_KV_HEADS * SB_HEAD_DIM
P_GATE = P_SBV + SB_KV_HEADS * SB_HEAD_DIM
P_KR = P_GATE + N_BRANCH * D_MODEL
P_END = P_KR + LANES
QF = MLA_HEADS * LANES
VF = MLA_HEADS * V_HEAD


def _params(sem):
    return pltpu.CompilerParams(dimension_semantics=sem, vmem_limit_bytes=VMEM_LIMIT)


def _const_spec(shape):
    nd = len(shape)
    return pl.BlockSpec(shape, lambda *_: (0,) * nd, pipeline_mode=pl.Buffered(1))


def _rms(x, g):
    return x * lax.rsqrt(jnp.mean(x * x, axis=-1, keepdims=True) + EPS) * g


def _sigmoid(x):
    return 1.0 / (1.0 + jnp.exp(-x))


def _mm(a, b):
    if b.dtype == F32:
        return jnp.dot(a.astype(F32), b, preferred_element_type=F32, precision=lax.Precision.HIGHEST)
    return jnp.dot(a.astype(BF16), b, preferred_element_type=F32)


def _rope_groups(x, rope_ref, reps):
    tc, ts1, ts2 = rope_ref[0], rope_ref[1], rope_ref[2]
    if reps > 1:
        tc = jnp.concatenate([tc] * reps, axis=1)
        ts1 = jnp.concatenate([ts1] * reps, axis=1)
        ts2 = jnp.concatenate([ts2] * reps, axis=1)
    n = x.shape[1]
    return x * tc + pltpu.roll(x, HALF, 1) * ts1 + pltpu.roll(x, n - HALF, 1) * ts2


def _in_kernel(x_ref, g_ref, w_ref, qg_ref, wuq_ref, kvg_ref, wukv_ref, rope_ref,
               glu_ref, q_ref, ckv_ref, kr_ref, k_ref, v_ref, sbq_ref, sbk_ref, sbv_ref,
               sbkb_ref, sbvb_ref, gate_ref, *, q_scale):
    tm = x_ref.shape[0]
    act = q_ref.dtype
    hb = _rms(x_ref[...], g_ref[...]).astype(act)

    def mm(lo, hi):
        return _mm(hb, w_ref[:, lo:hi])

    z = mm(P_GLU, P_CQ)
    glu_ref[...] = z[:, :C_CONV] * _sigmoid(z[:, C_CONV:])

    q = _mm(_rms(mm(P_CQ, P_CKV), qg_ref[...]), wuq_ref[...])
    q_ref[...] = (_rope_groups(q, rope_ref, MLA_HEADS) * q_scale).astype(act)

    ckv = _rms(mm(P_CKV, P_SBQ), kvg_ref[...])
    ckv_ref[...] = ckv
    kv = _mm(ckv, wukv_ref[...])
    kr = _rope_groups(mm(P_KR, P_END), rope_ref, 1)
    kr_ref[...] = kr[:, QK_NOPE:QK_NOPE + QK_ROPE]
    k_ref[...] = (kv[:, :QF] + jnp.concatenate([kr] * MLA_HEADS, axis=1)).astype(act)
    v_ref[...] = kv[:, QF:].astype(act)

    sbq = mm(P_SBQ, P_SBK) * SB_SCALE
    low = lax.broadcasted_iota(jnp.int32, (tm, LANES), 1) < SB_HEAD_DIM
    for h in range(SB_HEADS):
        c = sbq[:, (h // 2) * LANES:(h // 2 + 1) * LANES]
        in_low = h % 2 == 0
        want_low = h < SB_GROUP
        if in_low != want_low:
            c = pltpu.roll(c, SB_HEAD_DIM, 1)
        c = jnp.where(low, c, 0.0) if want_low else jnp.where(low, 0.0, c)
        sbq_ref[:, h * LANES:(h + 1) * LANES] = c.astype(act)

    sbk = mm(P_SBK, P_SBV)
    sbk_ref[...] = sbk
    sbkb_ref[...] = sbk.astype(act)
    sbv = mm(P_SBV, P_GATE)
    sbv_ref[...] = sbv
    sbvb_ref[...] = sbv.astype(act)

    gate_ref[...] = _sigmoid(mm(P_GATE, P_KR)).astype(gate_ref.dtype)


def _in_proj(x, g, w, qg, wuq, kvg, wukv, rope, tm, q_scale, gate_dtype):
    m = x.shape[0]
    n_rope = rope.shape[1] // tm
    row = lambda n: pl.BlockSpec((tm, n), lambda i: (i, 0))
    act = w.dtype
    outs = [
        (C_CONV, F32),
        (QF, act),
        (KV_LORA, F32),
        (QK_ROPE, F32),
        (QF, act),
        (VF, act),
        (SB_HEADS * LANES, act),
        (LANES, F32), (LANES, F32),
        (LANES, act), (LANES, act),
        (N_BRANCH * D_MODEL, gate_dtype),
    ]
    return pl.pallas_call(
        functools.partial(_in_kernel, q_scale=q_scale),
        grid=(m // tm,),
        in_specs=[row(D_MODEL), _const_spec((1, D_MODEL)), _const_spec((D_MODEL, P_END)),
                  _const_spec((1, Q_LORA)), _const_spec((Q_LORA, QF)),
                  _const_spec((1, KV_LORA)), _const_spec((KV_LORA, QF + VF)),
                  pl.BlockSpec((3, tm, LANES), lambda i: (0, i % n_rope, 0))],
        out_specs=[row(n) for n, _ in outs],
        out_shape=[jax.ShapeDtypeStruct((m, n), dt) for n, dt in outs],
        compiler_params=_params(("parallel",)),
        name="in_proj",
    )(x, g, w, qg, wuq, kvg, wukv, rope)


CONV_HALO = 32
CONV_ROWS = 32


def _ln_silu(y, b_ref, lg_ref, lb_ref):
    y = y + b_ref[...]
    yc = y - jnp.mean(y, axis=-1, keepdims=True)
    yn = yc * lax.rsqrt(jnp.mean(yc * yc, axis=-1, keepdims=True) + EPS) * lg_ref[...] + lb_ref[...]
    return yn * _sigmoid(yn)


def _conv_kernel(prev_ref, cur_ref, w_ref, b_ref, lg_ref, lb_ref, o_ref, win_ref):
    ts = cur_ref.shape[0]
    first = pl.program_id(1) == 0
    win_ref[0:CONV_HALO, :] = jnp.where(first, 0.0, prev_ref[...])
    win_ref[CONV_HALO:, :] = cur_ref[...]
    lead = CONV_HALO - (CONV_W - 1)

    n = CONV_ROWS + CONV_HALO

    def chunk(c, carry):
        r0 = pl.multiple_of(c * CONV_ROWS, CONV_ROWS)
        w0 = win_ref[pl.ds(r0, n), :]
        acc = jnp.zeros((CONV_ROWS, C_CONV), F32)
        for r in range(SUBLANES):
            wr = w0 if r == 0 else pltpu.roll(w0, n - r, 0)
            for k in range(CONV_W):
                if (lead + k) % SUBLANES == r:
                    a = (lead + k) // SUBLANES * SUBLANES
                    acc = acc + w_ref[k:k + 1, :] * wr[a:a + CONV_ROWS, :]
        o_ref[pl.ds(r0, CONV_ROWS), :] = _ln_silu(acc, b_ref, lg_ref, lb_ref).astype(BF16)
        return carry

    lax.fori_loop(0, ts // CONV_ROWS, chunk, 0)


def _conv_prompt(glu, w, b, lg, lb, batch, ts):
    m = glu.shape[0]
    nt = m // batch // ts
    per = ts // CONV_HALO
    return pl.pallas_call(
        _conv_kernel,
        grid=(batch, nt),
        in_specs=[pl.BlockSpec((CONV_HALO, C_CONV), lambda bb, i: (jnp.maximum((bb * nt + i) * per - 1, 0), 0)),
                  pl.BlockSpec((ts, C_CONV), lambda bb, i: (bb * nt + i, 0)),
                  pl.BlockSpec((CONV_W, C_CONV), lambda bb, i: (0, 0)),
                  pl.BlockSpec((1, C_CONV), lambda bb, i: (0, 0)),
                  pl.BlockSpec((1, C_CONV), lambda bb, i: (0, 0)),
                  pl.BlockSpec((1, C_CONV), lambda bb, i: (0, 0))],
        out_specs=pl.BlockSpec((ts, C_CONV), lambda bb, i: (bb * nt + i, 0)),
        out_shape=jax.ShapeDtypeStruct((m, C_CONV), BF16),
        scratch_shapes=[pltpu.VMEM((ts + CONV_HALO, C_CONV), F32)],
        compiler_params=_params(("parallel", "arbitrary")),
        name="conv_prompt",
    )(glu, glu, w, b, lg, lb)


def _conv_sample_kernel(st_ref, glu_ref, w_ref, b_ref, lg_ref, lb_ref, o_ref):
    acc = w_ref[CONV_W - 1:CONV_W, :] * glu_ref[...]
    for k in range(CONV_W - 1):
        acc = acc + w_ref[k:k + 1, :] * st_ref[k]
    o_ref[...] = _ln_silu(acc, b_ref, lg_ref, lb_ref)


def _conv_sample(state_t, glu, w, b, lg, lb):
    n = glu.shape[0]
    full = lambda s: pl.BlockSpec(s, lambda i: (0,) * len(s))
    return pl.pallas_call(
        _conv_sample_kernel,
        grid=(1,),
        in_specs=[full((CONV_W - 1, n, C_CONV)), full((n, C_CONV)), full((CONV_W, C_CONV)),
                  full((1, C_CONV)), full((1, C_CONV)), full((1, C_CONV))],
        out_specs=full((n, C_CONV)),
        out_shape=jax.ShapeDtypeStruct((n, C_CONV), F32),
        compiler_params=_params(("arbitrary",)),
        name="conv_sample",
    )(state_t, glu, w, b, lg, lb)


def _mla_kernel(q_ref, k_ref, v_ref, o_ref, m_ref, l_ref, acc_ref, *, t):
    qi = pl.program_id(2)
    row = lax.broadcasted_iota(jnp.int32, (t, t), 0)
    col = lax.broadcasted_iota(jnp.int32, (t, t), 1)
    outs = []
    for hh in range(2):
        q = q_ref[:, hh * LANES:(hh + 1) * LANES]
        m_ref[...] = jnp.full((t, 1), -jnp.inf, F32)
        l_ref[...] = jnp.zeros((t, 1), F32)
        acc_ref[...] = jnp.zeros((t, LANES), F32)

        def block(kb, diag):
            ks = pl.multiple_of(kb * t, t)
            k = k_ref[pl.ds(ks, t), hh * LANES:(hh + 1) * LANES]
            s = lax.dot_general(q, k, NT_DIMS, preferred_element_type=F32)
            if diag:
                s = jnp.where(col <= row, s, -jnp.inf)
            m_old = m_ref[...]
            m_new = jnp.maximum(m_old, jnp.max(s, axis=-1, keepdims=True))
            alpha = jnp.exp(m_old - m_new)
            p = jnp.exp(s - m_new)
            l_ref[...] = alpha * l_ref[...] + jnp.sum(p, axis=-1, keepdims=True)
            acc_ref[...] = alpha * acc_ref[...] + jnp.dot(p.astype(BF16), v_ref[pl.ds(ks, t), :],
                                                          preferred_element_type=F32)
            m_ref[...] = m_new

        def body(kb, carry):
            block(kb, False)
            return carry

        lax.fori_loop(0, qi, body, 0)
        block(qi, True)
        outs.append(acc_ref[...] / l_ref[...])
    low = lax.broadcasted_iota(jnp.int32, (t, LANES), 1) < V_HEAD
    o_ref[...] = jnp.where(low, outs[0], outs[1]).astype(BF16)


def _mla_prompt(q, k, v, batch, t):
    m = q.shape[0]
    s = m // batch
    nq = s // t
    pairs = MLA_HEADS // 2
    return pl.pallas_call(
        functools.partial(_mla_kernel, t=t),
        grid=(batch, pairs, nq),
        in_specs=[pl.BlockSpec((t, 2 * LANES), lambda b, p, i: (b * nq + i, p)),
                  pl.BlockSpec((s, 2 * LANES), lambda b, p, i: (b, p)),
                  pl.BlockSpec((s, LANES), lambda b, p, i: (b, p))],
        out_specs=pl.BlockSpec((t, LANES), lambda b, p, i: (b * nq + i, p)),
        out_shape=jax.ShapeDtypeStruct((m, VF), BF16),
        scratch_shapes=[pltpu.VMEM((t, 1), F32), pltpu.VMEM((t, 1), F32), pltpu.VMEM((t, LANES), F32)],
        compiler_params=_params(("parallel", "parallel", "arbitrary")),
        name="mla_prompt",
    )(q, k, v)


def _softplus(z):
    return jnp.maximum(z, 0.0) + jnp.log1p(jnp.exp(-jnp.abs(z)))


def _split(x):
    hi = x.astype(BF16)
    return hi, (x - hi.astype(F32)).astype(BF16)


def _mm3(a, b_hi, b_lo, dims):
    n = a.shape[0]
    a_hi = a.astype(BF16).astype(F32)
    stacked = jnp.concatenate([a_hi, a - a_hi], axis=0).astype(BF16)
    r = lax.dot_general(stacked, b_hi, dims, preferred_element_type=F32)
    return r[:n] + r[n:] + lax.dot_general(a.astype(BF16), b_lo, dims, preferred_element_type=F32)


def _sb_block(q, k, v, u, carry, mask):
    exact = isinstance(k, tuple)
    z = _mm3(q, *k, NT_DIMS) if exact else lax.dot_general(q, k, NT_DIMS, preferred_element_type=F32)
    log_rest = -_softplus(z)
    if mask is not None:
        log_rest = jnp.where(mask, log_rest, 0.0)
    hi, lo = _split(log_rest)
    after = (jnp.dot(hi, u, preferred_element_type=F32) + jnp.dot(lo, u, preferred_element_type=F32)) + carry
    a = jnp.exp(z + log_rest + after)
    if mask is not None:
        a = jnp.where(mask, a, 0.0)
    o = _mm3(a, *v, NN_DIMS) if exact else jnp.dot(a.astype(BF16), v, preferred_element_type=F32)
    return o, jnp.sum(log_rest, axis=-1, keepdims=True)


def _sb_kernel(q_ref, k_ref, v_ref, u_ref, o_ref, carry_ref, acc_ref, *, t):
    qi = pl.program_id(1)
    strict = lax.broadcasted_iota(jnp.int32, (t, t), 1) < lax.broadcasted_iota(jnp.int32, (t, t), 0)
    low = lax.broadcasted_iota(jnp.int32, (t, LANES), 1) < SB_HEAD_DIM
    u = u_ref[...]
    outs = []
    for h in range(SB_HEADS):
        q = q_ref[:, h * LANES:(h + 1) * LANES]

        def block(kb, mask):
            ks = pl.multiple_of(kb * t, t)
            o, s = _sb_block(q, k_ref[pl.ds(ks, t), :], v_ref[pl.ds(ks, t), :], u, carry_ref[...], mask)
            acc_ref[...] += o
            carry_ref[...] += s

        carry_ref[...] = jnp.zeros((t, 1), F32)
        acc_ref[...] = jnp.zeros((t, LANES), F32)
        block(qi, strict)

        def cond(st):
            return jnp.logical_and(st[0] >= 0, st[1] > SB_DEAD_LOG)

        def body(st):
            block(st[0], None)
            return st[0] - 1, jnp.max(carry_ref[...])

        lax.while_loop(cond, body, (qi - 1, jnp.max(carry_ref[...])))
        o = acc_ref[...]
        in_low = h < SB_GROUP
        want_low = h % 2 == 0
        outs.append(o if in_low == want_low else pltpu.roll(o, SB_HEAD_DIM, 1))
    for p in range(SB_HEADS // 2):
        o_ref[:, p * LANES:(p + 1) * LANES] = jnp.where(low, outs[2 * p], outs[2 * p + 1]).astype(BF16)


def _upper_ones(t):
    r = np.arange(t)
    return jnp.asarray((r[:, None] > r[None, :]).astype(np.float32), dtype=BF16)


def _sb_prompt(q, k, v, batch, t):
    m = q.shape[0]
    s = m // batch
    nq = s // t
    return pl.pallas_call(
        functools.partial(_sb_kernel, t=t),
        grid=(batch, nq),
        in_specs=[pl.BlockSpec((t, SB_HEADS * LANES), lambda b, i: (b * nq + i, 0)),
                  pl.BlockSpec((s, LANES), lambda b, i: (b, 0)),
                  pl.BlockSpec((s, LANES), lambda b, i: (b, 0)),
                  pl.BlockSpec((t, t), lambda b, i: (0, 0))],
        out_specs=pl.BlockSpec((t, SB_HEADS * SB_HEAD_DIM), lambda b, i: (b * nq + i, 0)),
        out_shape=jax.ShapeDtypeStruct((m, SB_HEADS * SB_HEAD_DIM), BF16),
        scratch_shapes=[pltpu.VMEM((t, 1), F32), pltpu.VMEM((t, LANES), F32)],
        compiler_params=_params(("parallel", "arbitrary")),
        name="sb_prompt",
    )(q, k, v, _upper_ones(t))


def _head_mm_kernel(x_ref, w_ref, o_ref):
    o_ref[0] = _mm(x_ref[...], w_ref[0])


def _absorb_q(q, wuk_t):
    n = q.shape[0]
    return pl.pallas_call(
        _head_mm_kernel,
        grid=(MLA_HEADS,),
        in_specs=[pl.BlockSpec((n, LANES), lambda h: (0, h)),
                  pl.BlockSpec((1, LANES, KV_LORA), lambda h: (h, 0, 0))],
        out_specs=pl.BlockSpec((1, n, KV_LORA), lambda h: (h, 0, 0)),
        out_shape=jax.ShapeDtypeStruct((MLA_HEADS, n, KV_LORA), F32),
        compiler_params=_params(("parallel",)),
        name="absorb_q",
    )(q, wuk_t)


def _unabsorb_kernel(x_ref, w_ref, o_ref):
    o_ref[0] = _mm(x_ref[0], w_ref[0])


def _unabsorb_o(o_lat, wuv):
    n = o_lat.shape[1]
    return pl.pallas_call(
        _unabsorb_kernel,
        grid=(MLA_HEADS,),
        in_specs=[pl.BlockSpec((1, n, KV_LORA), lambda h: (h, 0, 0)),
                  pl.BlockSpec((1, KV_LORA, V_HEAD), lambda h: (h, 0, 0))],
        out_specs=pl.BlockSpec((1, n, V_HEAD), lambda h: (h, 0, 0)),
        out_shape=jax.ShapeDtypeStruct((MLA_HEADS, n, V_HEAD), F32),
        compiler_params=_params(("parallel",)),
        name="unabsorb_o",
    )(o_lat, wuv)


PAGES_PER_CHUNK = 16
CHUNK = PAGES_PER_CHUNK * PAGE_SIZE
SB_SUB = 256


def _paged_kernel(pt_ref, qlat_ref, qrope_ref, sbq_ref, ckv_ref, krn_ref, u_ref,
                  lat_hbm, kr_hbm, sbk_hbm, sbv_hbm,
                  olat_ref, osb_ref,
                  lat_buf, kr_buf, sbk_buf, sbv_buf, sem,
                  s_ref, carry_ref, sacc_ref, *, layer, n_pages):
    b = pl.program_id(0)
    c = pl.program_id(1)
    n_chunks = n_pages // PAGES_PER_CHUNK
    step = b * n_chunks + c
    n_steps = pl.num_programs(0) * n_chunks

    def copies(st):
        bb = st // n_chunks
        cc = n_chunks - 1 - st % n_chunks
        out = []
        for p in range(PAGES_PER_CHUNK):
            page = pt_ref[bb * n_pages + cc * PAGES_PER_CHUNK + p]
            seq_rows = pl.ds(cc * CHUNK + p * PAGE_SIZE, PAGE_SIZE)
            step_rows = pl.ds(p * PAGE_SIZE, PAGE_SIZE)
            for i, (hbm, dst) in enumerate(((lat_hbm, lat_buf.at[bb % 2, seq_rows]),
                                            (kr_hbm, kr_buf.at[bb % 2, seq_rows]),
                                            (sbk_hbm, sbk_buf.at[st % 2, step_rows]),
                                            (sbv_hbm, sbv_buf.at[st % 2, step_rows]))):
                out.append(pltpu.make_async_copy(hbm.at[layer, page], dst, sem.at[i, st % 2]))
        return out

    @pl.when(step == 0)
    def _():
        for cp in copies(step):
            cp.start()

    @pl.when(step + 1 < n_steps)
    def _():
        for cp in copies(step + 1):
            cp.start()

    for cp in copies(step):
        cp.wait()

    seq = b % 2
    slot = step % 2
    cc = n_chunks - 1 - c
    qlat = qlat_ref[0].astype(BF16)
    qrope = qrope_ref[0].astype(BF16)

    @pl.when(c == 0)
    def _():
        carry_ref[...] = jnp.zeros_like(carry_ref)
        sacc_ref[...] = jnp.zeros_like(sacc_ref)

    base = pl.multiple_of(cc * CHUNK, CHUNK)
    lat_c = lat_buf[seq, pl.ds(base, CHUNK), :].astype(BF16)
    kr_c = kr_buf[seq, pl.ds(base, CHUNK), :].astype(BF16)
    s_ref[cc] = (lax.dot_general(qlat, lat_c, NT_DIMS, preferred_element_type=F32)
                 + lax.dot_general(qrope, kr_c, NT_DIMS, preferred_element_type=F32)) * MLA_SCALE

    sq = sbq_ref[0]
    u = u_ref[...]
    for j in reversed(range(CHUNK // SB_SUB)):
        ks = sbk_buf[slot, j * SB_SUB:(j + 1) * SB_SUB, :].astype(BF16)
        vs = sbv_buf[slot, j * SB_SUB:(j + 1) * SB_SUB, :].astype(BF16)
        o, rs = _sb_block(sq, ks, vs, u, carry_ref[...], None)
        sacc_ref[...] += o
        carry_ref[...] += rs

    @pl.when(c == n_chunks - 1)
    def _():
        ckv_n = ckv_ref[0].astype(BF16).astype(F32)
        kr_n = krn_ref[0].astype(BF16).astype(F32)
        s_new = (jnp.sum(qlat.astype(F32) * ckv_n, axis=-1, keepdims=True)
                 + jnp.sum(qrope.astype(F32) * kr_n, axis=-1, keepdims=True)) * MLA_SCALE
        s_all = s_ref[...]
        m = jnp.maximum(jnp.max(jnp.max(s_all, axis=-1, keepdims=True), axis=0), s_new)
        e_all = jnp.exp(s_all - m)
        e_new = jnp.exp(s_new - m)
        denom = jnp.sum(jnp.sum(e_all, axis=-1, keepdims=True), axis=0) + e_new
        acc = (e_new / denom).astype(BF16).astype(F32) * ckv_n
        for k in range(n_chunks):
            p = (e_all[k] / denom).astype(BF16)
            acc = acc + jnp.dot(p, lat_buf[seq, k * CHUNK:(k + 1) * CHUNK, :].astype(BF16),
                                preferred_element_type=F32)
        olat_ref[0] = acc
        o = sacc_ref[...]
        upper = lax.broadcasted_iota(jnp.int32, o.shape, 0) >= SB_GROUP
        o = jnp.where(upper, pltpu.roll(o, SB_HEAD_DIM, 1), o)
        osb_ref[0] = o[:, :SB_HEAD_DIM]


def _paged_attention(page_table, qlat, qrope, sbq, ckv, krn, lat_c, kr_c, sbk_c, sbv_c, layer):
    n, n_pages = page_table.shape
    n_chunks = n_pages // PAGES_PER_CHUNK
    per_seq = lambda s: pl.BlockSpec((1,) + s, lambda b, c, pt: (b,) + (0,) * len(s))
    anyspec = pl.BlockSpec(memory_space=pl.ANY)
    grid_spec = pltpu.PrefetchScalarGridSpec(
        num_scalar_prefetch=1,
        grid=(n, n_chunks),
        in_specs=[per_seq((MLA_HEADS, KV_LORA)), per_seq((MLA_HEADS, QK_ROPE)), per_seq((SB_HEADS, LANES)),
                  per_seq((1, KV_LORA)), per_seq((1, QK_ROPE)),
                  pl.BlockSpec((SB_SUB, SB_SUB), lambda b, c, pt: (0, 0)),
                  anyspec, anyspec, anyspec, anyspec],
        out_specs=[per_seq((MLA_HEADS, KV_LORA)), per_seq((SB_HEADS, SB_HEAD_DIM))],
        scratch_shapes=[pltpu.VMEM((2, n_pages * PAGE_SIZE, KV_LORA), F32),
                        pltpu.VMEM((2, n_pages * PAGE_SIZE, QK_ROPE), F32),
                        pltpu.VMEM((2, CHUNK, LANES), F32), pltpu.VMEM((2, CHUNK, LANES), F32),
                        pltpu.SemaphoreType.DMA((4, 2)),
                        pltpu.VMEM((n_chunks, MLA_HEADS, CHUNK), F32),
                        pltpu.VMEM((SB_HEADS, 1), F32), pltpu.VMEM((SB_HEADS, LANES), F32)],
    )
    return pl.pallas_call(
        functools.partial(_paged_kernel, layer=layer, n_pages=n_pages),
        grid_spec=grid_spec,
        out_shape=[jax.ShapeDtypeStruct((n, MLA_HEADS, KV_LORA), F32),
                   jax.ShapeDtypeStruct((n, SB_HEADS, SB_HEAD_DIM), F32)],
        compiler_params=_params(("arbitrary", "arbitrary")),
        name="paged_attention",
    )(page_table.reshape(-1), qlat, qrope, sbq, ckv, krn, _upper_ones(SB_SUB), lat_c, kr_c, sbk_c, sbv_c)


def _merge_kernel(x_ref, co_ref, mo_ref, so_ref, gt_ref, wc_ref, wm_ref, ws_ref, wo_ref, g2_ref, *rest, moe):
    if moe:
        rt_ref, xo_ref, h_ref, gate_ref = rest
    else:
        xo_ref, h_ref = rest
    d = D_MODEL
    m = (gt_ref[:, 0:d].astype(F32) * _mm(co_ref[...], wc_ref[...])
         + gt_ref[:, d:2 * d].astype(F32) * _mm(mo_ref[...], wm_ref[...])
         + gt_ref[:, 2 * d:3 * d].astype(F32) * _mm(so_ref[...], ws_ref[...]))
    xn = x_ref[...] + _mm(m, wo_ref[...])
    xo_ref[...] = xn
    h = _rms(xn, g2_ref[...])
    h_ref[...] = h.astype(h_ref.dtype)
    if moe:
        logits = _mm(h, rt_ref[...])
        lane = lax.broadcasted_iota(jnp.int32, logits.shape, 1)
        logits = jnp.where(lane < N_EXPERTS, logits, -jnp.inf)
        m1 = jnp.max(logits, axis=-1, keepdims=True)
        i1 = jnp.min(jnp.where(logits == m1, lane, LANES), axis=-1, keepdims=True)
        rest_l = jnp.where(lane == i1, -jnp.inf, logits)
        m2 = jnp.max(rest_l, axis=-1, keepdims=True)
        i2 = jnp.min(jnp.where(rest_l == m2, lane, LANES), axis=-1, keepdims=True)
        e = jnp.exp(m2 - m1)
        w1 = 1.0 / (1.0 + e)
        gate_ref[...] = jnp.where(lane == i1, w1, 0.0) + jnp.where(lane == i2, e * w1, 0.0)


def _merge(x, co, mo, so, gt, wc, wm, ws, wo, g2, router, tm):
    m = x.shape[0]
    row = lambda n: pl.BlockSpec((tm, n), lambda i: (i, 0))
    moe = router is not None
    ins = [x, co, mo, so, gt, wc, wm, ws, wo, g2]
    in_specs = [row(D_MODEL), row(C_CONV), row(VF), row(SB_HEADS * SB_HEAD_DIM), row(N_BRANCH * D_MODEL),
                _const_spec((C_CONV, D_MODEL)), _const_spec((VF, D_MODEL)),
                _const_spec((SB_HEADS * SB_HEAD_DIM, D_MODEL)), _const_spec((D_MODEL, D_MODEL)),
                _const_spec((1, D_MODEL))]
    out_specs = [row(D_MODEL), row(D_MODEL)]
    out_shape = [jax.ShapeDtypeStruct((m, D_MODEL), F32), jax.ShapeDtypeStruct((m, D_MODEL), wo.dtype)]
    if moe:
        ins.append(router)
        in_specs.append(_const_spec((D_MODEL, LANES)))
        out_specs.append(row(LANES))
        out_shape.append(jax.ShapeDtypeStruct((m, LANES), F32))
    return pl.pallas_call(
        functools.partial(_merge_kernel, moe=moe),
        grid=(m // tm,),
        in_specs=in_specs, out_specs=out_specs, out_shape=out_shape,
        compiler_params=_params(("parallel",)),
        name="merge_moe" if moe else "merge",
    )(*ins)


def _swiglu_acc(h, wg, wu, wd):
    g = _mm(h, wg)
    u = _mm(h, wu)
    return _mm(g * _sigmoid(g) * u, wd)


def _ffn_kernel(h_ref, x_ref, wg_ref, wu_ref, wd_ref, o_ref, acc_ref):
    f = pl.program_id(1)

    @pl.when(f == 0)
    def _():
        acc_ref[...] = jnp.zeros_like(acc_ref)

    acc_ref[...] += _swiglu_acc(h_ref[...], wg_ref[...], wu_ref[...], wd_ref[...])

    @pl.when(f == pl.num_programs(1) - 1)
    def _():
        o_ref[...] = x_ref[...] + acc_ref[...]


def _ffn(h, x, wg, wu, wd, tm, tf):
    m = h.shape[0]
    return pl.pallas_call(
        _ffn_kernel,
        grid=(m // tm, D_FF // tf),
        in_specs=[pl.BlockSpec((tm, D_MODEL), lambda i, f: (i, 0)),
                  pl.BlockSpec((tm, D_MODEL), lambda i, f: (i, 0)),
                  pl.BlockSpec((D_MODEL, tf), lambda i, f: (0, f)),
                  pl.BlockSpec((D_MODEL, tf), lambda i, f: (0, f)),
                  pl.BlockSpec((tf, D_MODEL), lambda i, f: (f, 0))],
        out_specs=pl.BlockSpec((tm, D_MODEL), lambda i, f: (i, 0)),
        out_shape=jax.ShapeDtypeStruct((m, D_MODEL), F32),
        scratch_shapes=[pltpu.VMEM((tm, D_MODEL), F32)],
        compiler_params=_params(("parallel", "arbitrary")),
        name="ffn",
    )(h, x, wg, wu, wd)


def _moe_kernel(h_ref, x_ref, gate_ref, wg_ref, wu_ref, wd_ref, o_ref, acc_ref):
    e = pl.program_id(1)
    f = pl.program_id(2)

    @pl.when(jnp.logical_and(e == 0, f == 0))
    def _():
        acc_ref[...] = jnp.zeros_like(acc_ref)

    gate = gate_ref[...]
    lane = lax.broadcasted_iota(jnp.int32, gate.shape, 1)
    ge = jnp.sum(jnp.where(lane == e, gate, 0.0), axis=-1, keepdims=True)
    acc_ref[...] += ge * _swiglu_acc(h_ref[...], wg_ref[0], wu_ref[0], wd_ref[0])

    @pl.when(jnp.logical_and(e == pl.num_programs(1) - 1, f == pl.num_programs(2) - 1))
    def _():
        o_ref[...] = x_ref[...] + acc_ref[...]


def _moe(h, x, gate, wg, wu, wd, tm, tf):
    m = h.shape[0]
    return pl.pallas_call(
        _moe_kernel,
        grid=(m // tm, N_EXPERTS, D_EXPERT // tf),
        in_specs=[pl.BlockSpec((tm, D_MODEL), lambda i, e, f: (i, 0)),
                  pl.BlockSpec((tm, D_MODEL), lambda i, e, f: (i, 0)),
                  pl.BlockSpec((tm, LANES), lambda i, e, f: (i, 0)),
                  pl.BlockSpec((1, D_MODEL, tf), lambda i, e, f: (e, 0, f)),
                  pl.BlockSpec((1, D_MODEL, tf), lambda i, e, f: (e, 0, f)),
                  pl.BlockSpec((1, tf, D_MODEL), lambda i, e, f: (e, f, 0))],
        out_specs=pl.BlockSpec((tm, D_MODEL), lambda i, e, f: (i, 0)),
        out_shape=jax.ShapeDtypeStruct((m, D_MODEL), F32),
        scratch_shapes=[pltpu.VMEM((tm, D_MODEL), F32)],
        compiler_params=_params(("parallel", "arbitrary", "arbitrary")),
        name="moe",
    )(h, x, gate, wg, wu, wd)


def _final_norm_kernel(x_ref, g_ref, o_ref):
    o_ref[...] = _rms(x_ref[...], g_ref[...])


def _final_norm(x, g, tm):
    m = x.shape[0]
    return pl.pallas_call(
        _final_norm_kernel,
        grid=(m // tm,),
        in_specs=[pl.BlockSpec((tm, D_MODEL), lambda i: (i, 0)), pl.BlockSpec((1, D_MODEL), lambda i: (0, 0))],
        out_specs=pl.BlockSpec((tm, D_MODEL), lambda i: (i, 0)),
        out_shape=jax.ShapeDtypeStruct((m, D_MODEL), F32),
        compiler_params=_params(("parallel",)),
        name="final_norm",
    )(x, g)


def _pack_w_in(w_in):
    o = np.concatenate([[0], np.cumsum(IN_SIZES)])
    part = lambda i, j: w_in[:, :, o[i]:o[j]]
    zeros = lambda n: jnp.zeros(w_in.shape[:2] + (n,), w_in.dtype)
    kr = jnp.concatenate([zeros(QK_NOPE), part(4, 5), zeros(LANES - QK_NOPE - QK_ROPE)], axis=-1)
    return jnp.concatenate([part(0, 2), part(2, 3), part(3, 4), part(5, 6), part(6, 7), part(7, 8), part(8, 9), kr],
                           axis=-1)


def _pad_heads(w, width):
    pad = [(0, 0)] * (w.ndim - 1) + [(0, width - w.shape[-1])]
    w = jnp.pad(w, pad)
    return w.reshape(w.shape[:-2] + (w.shape[-2] * width,))


def _rope_tables(pos):
    inv = ROPE_BASE ** (-jnp.arange(HALF, dtype=F32) / HALF)
    ang = pos.astype(F32)[:, None] * inv[None, :]
    cos, sin = jnp.cos(ang), jnp.sin(ang)
    t = pos.shape[0]
    z = lambda n: jnp.zeros((t, n), F32)
    tail = LANES - QK_NOPE - QK_ROPE
    tc = jnp.concatenate([jnp.ones((t, QK_NOPE), F32), cos, cos, z(tail)], axis=1)
    ts1 = jnp.concatenate([z(QK_NOPE), z(HALF), sin, z(tail)], axis=1)
    ts2 = jnp.concatenate([z(QK_NOPE), -sin, z(HALF), z(tail)], axis=1)
    return jnp.stack([tc, ts1, ts2])


def kernel(x_prompt, x_sample, cache_mla_latent, cache_mla_krope, cache_sb_k, cache_sb_v, state_conv, page_table,
           norm_mix_g, norm_ffn_g, norm_final_g, w_in, conv_w, conv_b, conv_ln_g, conv_ln_b,
           mla_q_norm_g, mla_w_uq, mla_kv_norm_g, mla_w_uk, mla_w_uv,
           w_br_conv, w_br_mla, w_br_sb, w_out,
           ffn_w_gate, ffn_w_up, ffn_w_down, moe_router, moe_w_gate, moe_w_up, moe_w_down):
    batch, seq, d = x_prompt.shape
    n_dec = x_sample.shape[0]
    mp = batch * seq
    tm_p = 256
    t_attn = 256
    ts_conv = 512

    bf = lambda ws: tuple(w.astype(BF16) for w in ws)
    sel = lambda ws, j: tuple(w[j] for w in ws)
    in_w = bf((_pack_w_in(w_in),
               _pad_heads(mla_w_uq.reshape(DEPTH, Q_LORA, MLA_HEADS, QK_NOPE + QK_ROPE), LANES),
               jnp.concatenate([_pad_heads(mla_w_uk, LANES), mla_w_uv.reshape(DEPTH, KV_LORA, VF)], axis=-1)))
    wuk_t = jnp.pad(jnp.transpose(mla_w_uk, (0, 2, 3, 1)),
                    ((0, 0), (0, 0), (0, LANES - QK_NOPE), (0, 0))).astype(BF16)
    wuv_h = jnp.transpose(mla_w_uv, (0, 2, 1, 3)).astype(BF16)
    br_w = bf((w_br_conv, w_br_mla, w_br_sb, w_out))
    ffn_w = bf((ffn_w_gate, ffn_w_up, ffn_w_down))
    moe_w = bf((moe_w_gate, moe_w_up, moe_w_down))
    router_p = jnp.pad(moe_router, ((0, 0), (0, 0), (0, LANES - N_EXPERTS))).astype(BF16)
    row2 = lambda a: a[:, None, :]
    g_mix, g_ffn, g_q, g_kv = row2(norm_mix_g), row2(norm_ffn_g), row2(mla_q_norm_g), row2(mla_kv_norm_g)
    cb, clg, clb = row2(conv_b), row2(conv_ln_g), row2(conv_ln_b)
    rope_p = _rope_tables(jnp.arange(seq))
    rope_s = _rope_tables(jnp.full((n_dec,), PAST_LEN))
    state_t = jnp.transpose(state_conv, (0, 2, 1, 3))
    sbk_c = cache_sb_k.reshape(cache_sb_k.shape[:3] + (LANES,))
    sbv_c = cache_sb_v.reshape(cache_sb_v.shape[:3] + (LANES,))

    xp = x_prompt.reshape(mp, d)
    xs = x_sample.reshape(n_dec, d)
    outs_p = [[] for _ in range(5)]
    outs_s = [[] for _ in range(5)]
    for l in range(DEPTH):
        pin = (g_mix[l], in_w[0][l], g_q[l], in_w[1][l], g_kv[l], in_w[2][l])
        glu_p, q_p, ckv_p, kr_p, k_p, v_p, sbq_p, sbk_p, sbv_p, sbkb_p, sbvb_p, gt_p = _in_proj(
            xp, *pin, rope_p, tm_p, MLA_SCALE, BF16)
        glu_s, q_s, ckv_s, kr_s, _, _, sbq_s, sbk_s, sbv_s, _, _, gt_s = _in_proj(
            xs, *pin, rope_s, n_dec, 1.0, F32)

        conv_p = _conv_prompt(glu_p, conv_w[l], cb[l], clg[l], clb[l], batch, ts_conv)
        conv_s = _conv_sample(state_t[l], glu_s, conv_w[l], cb[l], clg[l], clb[l])

        mla_p = _mla_prompt(q_p, k_p, v_p, batch, t_attn)
        sb_p = _sb_prompt(sbq_p, sbkb_p, sbvb_p, batch, t_attn)

        qlat = jnp.transpose(_absorb_q(q_s, wuk_t[l]), (1, 0, 2))
        qrope = q_s.reshape(n_dec, MLA_HEADS, LANES)[:, :, QK_NOPE:QK_NOPE + QK_ROPE]
        olat, osb = _paged_attention(page_table, qlat, qrope, sbq_s.reshape(n_dec, SB_HEADS, LANES),
                                     ckv_s[:, None, :], kr_s[:, None, :],
                                     cache_mla_latent, cache_mla_krope, sbk_c, sbv_c, l)
        mla_s = jnp.transpose(_unabsorb_o(jnp.transpose(olat, (1, 0, 2)), wuv_h[l]), (1, 0, 2)).reshape(n_dec, VF)
        sb_s = osb.reshape(n_dec, SB_HEADS * SB_HEAD_DIM)

        moe = l % 2 == 1
        i = l // 2
        router = router_p[i] if moe else None
        res_p = _merge(xp, conv_p, mla_p, sb_p, gt_p, *sel(br_w, l), g_ffn[l], router, tm_p)
        res_s = _merge(xs, conv_s, mla_s, sb_s, gt_s, *sel(br_w, l), g_ffn[l], router, n_dec)
        if moe:
            xp = _moe(res_p[1], res_p[0], res_p[2], *sel(moe_w, i), 512, D_EXPERT)
            xs = _moe(res_s[1], res_s[0], res_s[2], *sel(moe_w, i), n_dec, D_EXPERT)
        else:
            xp = _ffn(res_p[1], res_p[0], *sel(ffn_w, i), 512, D_FF // 2)
            xs = _ffn(res_s[1], res_s[0], *sel(ffn_w, i), n_dec, D_FF // 2)

        cst_p = glu_p.reshape(batch, seq, C_CONV)[:, seq - (CONV_W - 1):]
        cst_s = jnp.concatenate([state_conv[l][:, 1:], glu_s[:, None, :]], axis=1)
        for dst, val in zip(outs_p, (ckv_p, kr_p, sbk_p, sbv_p)):
            dst.append(val)
        outs_p[4].append(cst_p)
        for dst, val in zip(outs_s, (ckv_s, kr_s, sbk_s, sbv_s)):
            dst.append(val)
        outs_s[4].append(cst_s)

    y_p = _final_norm(xp, norm_final_g[None, :], 512).reshape(batch, seq, d)
    y_s = _final_norm(xs, norm_final_g[None, :], n_dec).reshape(n_dec, 1, d)

    def stack(vals, lead, tail):
        return jnp.stack(vals).reshape((DEPTH,) + lead + tail)

    kvh = (SB_KV_HEADS, SB_HEAD_DIM)
    return (y_p, y_s,
            stack(outs_p[0], (batch, seq), (KV_LORA,)), stack(outs_p[1], (batch, seq), (QK_ROPE,)),
            stack(outs_p[2], (batch, seq), kvh), stack(outs_p[3], (batch, seq), kvh),
            jnp.stack(outs_p[4]),
            stack(outs_s[0], (n_dec, 1), (KV_LORA,)), stack(outs_s[1], (n_dec, 1), (QK_ROPE,)),
            stack(outs_s[2], (n_dec, 1), kvh), stack(outs_s[3], (n_dec, 1), kvh),
            jnp.stack(outs_s[4]))
```
